```python
import math
import jax, jax.numpy as jnp
from jax import lax
import numpy as np

D_MODEL = 1024
BATCH = 4
SEQ = 8192
DEPTH = 2

SSM_WIDTH = D_MODEL // 2
SSM_GROUP = 16
SSM_GROUPS = SSM_WIDTH // SSM_GROUP
SSM_STATE = 64
SSM_CHUNK_MAX = 512
DT_MIN = 1e-3
DT_MAX = 1e-1
ATTN_HEADS = 8
HEAD_DIM = 64
ATTN_WIDTH = ATTN_HEADS * HEAD_DIM
IDX_HEADS = 8
IDX_DIM = 64
IDX_SCALE = (IDX_HEADS * IDX_DIM) ** -0.5
TOPK_MAX = 256
Q_BLOCK = 128
RMS_EPS = 1e-6

IN_WIDTHS = (
    SSM_WIDTH,
    SSM_WIDTH,
    ATTN_WIDTH,
    HEAD_DIM,
    HEAD_DIM,
    ATTN_WIDTH,
    IDX_HEADS * IDX_DIM,
    IDX_DIM,
    IDX_HEADS,
    D_MODEL,
    D_MODEL,
)
IN_COLS = sum(IN_WIDTHS)
IN_SPLITS = tuple(int(s) for s in np.cumsum(IN_WIDTHS)[:-1])

kernel_name = "hybrid_s5_dsa_gated_parallel"


def rms_norm(x, g):
    x32 = x.astype(jnp.float32)
    y = x32 * lax.rsqrt(jnp.mean(x32 * x32, axis=-1, keepdims=True) + RMS_EPS)
    return (y * g.astype(jnp.float32)).astype(x.dtype)


def _complex_affine_combine(e1, e2):
    a1r, a1i, b1r, b1i = e1
    a2r, a2i, b2r, b2i = e2
    return (a1r * a2r - a1i * a2i,
            a1r * a2i + a1i * a2r,
            a2r * b1r - a2i * b1i + b2r,
            a2r * b1i + a2i * b1r + b2i)


def s5_mixer(u, a_re, a_im, b_re, b_im, c_re, c_im, d_skip, log_dt):
    bsz, seq_len, _ = u.shape
    f32 = jnp.float32
    ar, ai = a_re.astype(f32), a_im.astype(f32)
    dt = jnp.exp(log_dt.astype(f32))[:, None]
    mag = jnp.exp(ar * dt)
    abar_r, abar_i = mag * jnp.cos(ai * dt), mag * jnp.sin(ai * dt)
    den = ar * ar + ai * ai
    nr = abar_r - 1.0
    cr = (nr * ar + abar_i * ai) / den
    ci = (abar_i * ar - nr * ai) / den
    br, bi = b_re.astype(f32), b_im.astype(f32)
    bb_r = cr[..., None] * br - ci[..., None] * bi
    bb_i = cr[..., None] * bi + ci[..., None] * br
    cre, cim = c_re.astype(f32), c_im.astype(f32)

    chunk = math.gcd(seq_len, SSM_CHUNK_MAX)
    n_chunks = seq_len // chunk
    ug = u.astype(f32).reshape(bsz, n_chunks, chunk, SSM_GROUPS, SSM_GROUP)
    ug = ug.transpose(1, 0, 2, 3, 4)

    def chunk_step(h0, u_c):
        h0r, h0i = h0
        bu_r = jnp.einsum('btgp,gnp->btgn', u_c, bb_r)
        bu_i = jnp.einsum('btgp,gnp->btgn', u_c, bb_i)
        a_r = jnp.broadcast_to(abar_r, bu_r.shape)
        a_i = jnp.broadcast_to(abar_i, bu_i.shape)
        cum_r, cum_i, hl_r, hl_i = lax.associative_scan(
            _complex_affine_combine, (a_r, a_i, bu_r, bu_i), axis=1)
        h_r = hl_r + cum_r * h0r[:, None] - cum_i * h0i[:, None]
        h_i = hl_i + cum_r * h0i[:, None] + cum_i * h0r[:, None]
        y = (jnp.einsum('btgn,gpn->btgp', h_r, cre)
             - jnp.einsum('btgn,gpn->btgp', h_i, cim))
        return (h_r[:, -1], h_i[:, -1]), y

    h_init = (jnp.zeros((bsz, SSM_GROUPS, SSM_STATE), f32),
              jnp.zeros((bsz, SSM_GROUPS, SSM_STATE), f32))
    _, y = lax.scan(chunk_step, h_init, ug)
    y = y.transpose(1, 0, 2, 3, 4).reshape(bsz, seq_len, SSM_WIDTH)
    y = y + d_skip.astype(f32) * u.astype(f32)
    return y.astype(u.dtype)


def dsa_mixer(q, k, v, q_idx, k_idx, w_idx):
    bsz, seq_len = q.shape[0], q.shape[1]
    f32 = jnp.float32
    top_k = min(TOPK_MAX, seq_len // 4)
    n_blocks = seq_len // Q_BLOCK
    key_pos = jnp.arange(seq_len)
    k_idx32 = k_idx.astype(f32)
    attn_scale = HEAD_DIM ** -0.5

    def to_blocks(t):
        return t.reshape((bsz, n_blocks, Q_BLOCK) + t.shape[2:]).swapaxes(0, 1)

    def block(args):
        blk, qb, qib, wb = args
        q_pos = blk * Q_BLOCK + jnp.arange(Q_BLOCK)
        causal = key_pos[None, :] <= q_pos[:, None]
        s = jax.nn.relu(jnp.einsum('bqhd,bkd->bqhk', qib.astype(f32), k_idx32)) * IDX_SCALE
        score = jnp.einsum('bqh,bqhk->bqk', wb.astype(f32), s)
        score = jnp.where(causal[None], score, -jnp.inf)
        _, sel = lax.top_k(score, top_k)
        valid = sel <= q_pos[None, :, None]
        k_sel = jax.vmap(lambda kk, ii: kk[ii])(k, sel)
        v_sel = jax.vmap(lambda vv, ii: vv[ii])(v, sel)
        logits = jnp.einsum('bqhd,bqkd->bqhk', qb.astype(f32), k_sel.astype(f32)) * attn_scale
        logits = jnp.where(valid[:, :, None, :], logits, -jnp.inf)
        p = jax.nn.softmax(logits, axis=-1)
        o = jnp.einsum('bqhk,bqkd->bqhd', p, v_sel.astype(f32))
        return o.astype(qb.dtype)

    out = lax.map(block, (jnp.arange(n_blocks), to_blocks(q), to_blocks(q_idx), to_blocks(w_idx)))
    return out.swapaxes(0, 1).reshape(bsz, seq_len, ATTN_WIDTH)


def hybrid_layer(x, c, norm_g, w_mod, b_mod, w_in, a_re, a_im, b_re, b_im, c_re, c_im,
                 d_skip, log_dt, w_glu, b_glu, w_a_o, w_b_o, w_out):
    bsz, seq_len, _ = x.shape
    mod = jax.nn.silu(c) @ w_mod + b_mod
    shift, scale, gate = jnp.split(mod, 3, axis=-1)
    h = rms_norm(x, norm_g) * (1 + scale[:, None]) + shift[:, None]
    proj = h @ w_in
    xa, za, q, k, v, zb, qi, ki, wi, ga, gb = jnp.split(proj, IN_SPLITS, axis=-1)

    ya = jax.nn.gelu(s5_mixer(xa, a_re, a_im, b_re, b_im, c_re, c_im, d_skip, log_dt))
    g = ya @ w_glu + b_glu
    ya = g[..., :SSM_WIDTH] * jax.nn.sigmoid(g[..., SSM_WIDTH:])
    ya = (ya * jax.nn.silu(za)) @ w_a_o

    yb = dsa_mixer(q.reshape(bsz, seq_len, ATTN_HEADS, HEAD_DIM), k, v,
                   qi.reshape(bsz, seq_len, IDX_HEADS, IDX_DIM), ki, wi)
    yb = (yb * jax.nn.silu(zb)) @ w_b_o

    merged = jax.nn.sigmoid(ga) * ya + jax.nn.sigmoid(gb) * yb
    return x + gate[:, None] * (merged @ w_out)


def setup_inputs(seed: int = 0) -> dict:
    key = jax.random.key(seed)
    ks = jax.random.split(key, 20)
    f32 = jnp.float32
    G, N, P = SSM_GROUPS, SSM_STATE, SSM_GROUP

    def nrm(k, shape, std):
        return jax.random.normal(k, shape, f32) * std

    n_idx = jnp.arange(N, dtype=f32)
    return {
        "x": nrm(ks[0], (BATCH, SEQ, D_MODEL), 1.0),
        "c": nrm(ks[1], (BATCH, D_MODEL), 1.0),
        "norm_g": 1.0 + nrm(ks[2], (DEPTH, D_MODEL), 0.02),
        "w_mod": nrm(ks[3], (DEPTH, D_MODEL, 3 * D_MODEL), 0.5 * D_MODEL ** -0.5),
        "b_mod": nrm(ks[4], (DEPTH, 3 * D_MODEL), 0.01),
        "w_in": nrm(ks[5], (DEPTH, D_MODEL, IN_COLS), D_MODEL ** -0.5),
        "a_re": -0.5 + nrm(ks[6], (DEPTH, G, N), 0.01),
        "a_im": math.pi * n_idx + nrm(ks[7], (DEPTH, G, N), 0.01),
        "b_re": nrm(ks[8], (DEPTH, G, N, P), (2 * P) ** -0.5),
        "b_im": nrm(ks[9], (DEPTH, G, N, P), (2 * P) ** -0.5),
        "c_re": nrm(ks[10], (DEPTH, G, P, N), N ** -0.5),
        "c_im": nrm(ks[11], (DEPTH, G, P, N), N ** -0.5),
        "d_skip": nrm(ks[12], (DEPTH, SSM_WIDTH), 1.0),
        "log_dt": jax.random.uniform(ks[13], (DEPTH, G), f32,
                                     minval=math.log(DT_MIN), maxval=math.log(DT_MAX)),
        "w_glu": nrm(ks[14], (DEPTH, SSM_WIDTH, 2 * SSM_WIDTH), SSM_WIDTH ** -0.5),
        "b_glu": nrm(ks[15], (DEPTH, 2 * SSM_WIDTH), 0.01),
        "w_a_o": nrm(ks[16], (DEPTH, SSM_WIDTH, D_MODEL), SSM_WIDTH ** -0.5),
        "w_b_o": nrm(ks[17], (DEPTH, ATTN_WIDTH, D_MODEL), ATTN_WIDTH ** -0.5),
        "w_out": nrm(ks[18], (DEPTH, D_MODEL, D_MODEL), D_MODEL ** -0.5),
        "final_g": 1.0 + nrm(ks[19], (D_MODEL,), 0.02),
    }


def reference(x, c, norm_g, w_mod, b_mod, w_in, a_re, a_im, b_re, b_im, c_re, c_im,
              d_skip, log_dt, w_glu, b_glu, w_a_o, w_b_o, w_out, final_g):
    for l in range(DEPTH):
        x = hybrid_layer(x, c, norm_g[l], w_mod[l], b_mod[l], w_in[l], a_re[l], a_im[l],
                         b_re[l], b_im[l], c_re[l], c_im[l], d_skip[l], log_dt[l],
                         w_glu[l], b_glu[l], w_a_o[l], w_b_o[l], w_out[l])
    return rms_norm(x, final_g)
```

```python
import functools
import math

import jax
import jax.numpy as jnp
from jax import lax
from jax.experimental import pallas as pl
from jax.experimental.pallas import tpu as pltpu

F32 = jnp.float32
BF16 = jnp.bfloat16
I32 = jnp.int32

RMS_EPS = 1e-6
SSM_GROUP = 16
SSM_STATE = 64
SSM_CHUNK = 16
ATTN_HEADS = 8
HEAD_DIM = 64
IDX_HEADS = 8
IDX_DIM = 64
TOPK_MAX = 256

V7X_VMEM_LIMIT = 56 * 1024 * 1024
LANES = 128

INT_MIN = -2 ** 31
NEG_BIG = -1e30


def _cparams(sem):
    return pltpu.CompilerParams(dimension_semantics=sem, vmem_limit_bytes=V7X_VMEM_LIMIT)


def _sigmoid(x):
    return 1.0 / (1.0 + jnp.exp(-x))


def _dot(a, b, **kw):
    return jnp.dot(a, b, preferred_element_type=F32, **kw)


def _dot_nt(a, b):
    return lax.dot_general(a, b, (((1,), (1,)), ((), ())), preferred_element_type=F32)


def _mod_kernel(c_ref, w_ref, b_ref, o_ref):
    c = c_ref[...]
    sc = c * _sigmoid(c)
    o_ref[0] = _dot(sc, w_ref[0], precision=lax.Precision.HIGHEST) + b_ref[0]


def _modulation(c, w_mod, b_mod):
    depth, d, d3 = w_mod.shape
    bsz = c.shape[0]
    nj = d3 // d
    return pl.pallas_call(
        _mod_kernel,
        grid=(depth, nj),
        in_specs=[
            pl.BlockSpec((bsz, d), lambda l, j: (0, 0)),
            pl.BlockSpec((1, d, d), lambda l, j: (l, 0, j)),
            pl.BlockSpec((1, 1, d), lambda l, j: (l, 0, j)),
        ],
        out_specs=pl.BlockSpec((1, bsz, d), lambda l, j: (l, 0, j)),
        out_shape=jax.ShapeDtypeStruct((depth, bsz, d3), F32),
        compiler_params=_cparams(("arbitrary", "arbitrary")),
        name="adaln_mod",
    )(c, w_mod, b_mod.reshape(depth, 1, d3))


def _norm_mod(x, g, mod):
    ms = jnp.mean(x * x, axis=-1, keepdims=True)
    y = x * lax.rsqrt(ms + RMS_EPS) * g
    return y * (1.0 + mod[1:2, :]) + mod[0:1, :]


def _proj_kernel(x_ref, g_ref, mod_ref, w_ref, xa_ref, q_ref, qi_ref, sm_ref, *, widths):
    h = _norm_mod(x_ref[...], g_ref[...], mod_ref[0]).astype(BF16)
    p = _dot(h, w_ref[...])
    w_xa, w_q, w_qi, _ = widths
    o = 0
    xa_ref[...] = p[:, o:o + w_xa]
    o += w_xa
    q_ref[...] = (p[:, o:o + w_q] * (HEAD_DIM ** -0.5)).astype(BF16)
    o += w_q
    qi_ref[...] = p[:, o:o + w_qi].astype(BF16)
    o += w_qi
    sm_ref[...] = p[:, o:]


def _input_proj(x2, g, mod, w_cat, widths, rows_per_batch, tm):
    r, d = x2.shape
    w_xa, w_q, w_qi, w_sm = widths
    tiles_per_batch = rows_per_batch // tm
    row = lambda i: (i, 0)
    return pl.pallas_call(
        functools.partial(_proj_kernel, widths=widths),
        grid=(r // tm,),
        in_specs=[
            pl.BlockSpec((tm, d), row),
            pl.BlockSpec((1, d), lambda i: (0, 0)),
            pl.BlockSpec((1, 3, d), lambda i: (i // tiles_per_batch, 0, 0)),
            pl.BlockSpec(w_cat.shape, lambda i: (0, 0)),
        ],
        out_specs=[
            pl.BlockSpec((tm, w_xa), row),
            pl.BlockSpec((tm, w_q), row),
            pl.BlockSpec((tm, w_qi), row),
            pl.BlockSpec((tm, w_sm), row),
        ],
        out_shape=[
            jax.ShapeDtypeStruct((r, w_xa), F32),
            jax.ShapeDtypeStruct((r, w_q), BF16),
            jax.ShapeDtypeStruct((r, w_qi), BF16),
            jax.ShapeDtypeStruct((r, w_sm), F32),
        ],
        compiler_params=_cparams(("arbitrary",)),
        name="input_proj",
    )(x2, g, mod, w_cat)


def _s5w_kernel(arc_ref, aic_ref, arr_ref, air_ref, ldt_ref, br_ref, bi_ref, cr_ref, ci_ref,
                winp_ref, coutp_ref, kl_ref, lam_ref, *, chunk):
    dt = jnp.exp(ldt_ref[0])

    def discretise(ar, ai):
        mag = jnp.exp(ar * dt)
        return mag * jnp.cos(ai * dt), mag * jnp.sin(ai * dt)

    ar, ai = arc_ref[0], aic_ref[0]
    lr, li = discretise(ar, ai)
    den = ar * ar + ai * ai
    nr = lr - 1.0
    zr = (nr * ar + li * ai) / den
    zi = (li * ar - nr * ai) / den
    br, bi = br_ref[0], bi_ref[0]
    bbr = zr * br - zi * bi
    bbi = zr * bi + zi * br

    rr, ri = discretise(arr_ref[0], air_ref[0])
    cr, ci = cr_ref[0], ci_ref[0]

    pr, pi = jnp.ones_like(lr), jnp.zeros_like(li)
    qr, qi = jnp.ones_like(rr), jnp.zeros_like(ri)
    hi = lax.Precision.HIGHEST
    for k in range(chunk):
        winp_ref[0, k, 0] = pr * bbr - pi * bbi
        winp_ref[0, k, 1] = pr * bbi + pi * bbr
        ckr = cr * qr - ci * qi
        cki = cr * qi + ci * qr
        kl_ref[0, k] = _dot(ckr, bbr, precision=hi) - _dot(cki, bbi, precision=hi)
        pr, pi = pr * lr - pi * li, pr * li + pi * lr
        qr, qi = qr * rr - qi * ri, qr * ri + qi * rr
        coutp_ref[0, k, 0] = cr * qr - ci * qi
        coutp_ref[0, k, 1] = -(cr * qi + ci * qr)
    lam_ref[0, 0] = qr
    lam_ref[0, 1] = qi


def _s5_weights(a_re, a_im, b_re, b_im, c_re, c_im, log_dt, chunk):
    g, n = a_re.shape
    p = b_re.shape[-1]
    g3 = lambda i: (i, 0, 0)
    col = pl.BlockSpec((1, n, 1), g3)
    rowb = pl.BlockSpec((1, 1, n), g3)
    return pl.pallas_call(
        functools.partial(_s5w_kernel, chunk=chunk),
        grid=(g,),
        in_specs=[col, col, rowb, rowb, pl.BlockSpec((1, 1, 1), g3),
                  pl.BlockSpec((1, n, p), g3), pl.BlockSpec((1, n, p), g3),
                  pl.BlockSpec((1, p, n), g3), pl.BlockSpec((1, p, n), g3)],
        out_specs=[
            pl.BlockSpec((1, chunk, 2, n, p), lambda i: (i, 0, 0, 0, 0)),
            pl.BlockSpec((1, chunk, 2, p, n), lambda i: (i, 0, 0, 0, 0)),
            pl.BlockSpec((1, chunk, p, p), lambda i: (i, 0, 0, 0)),
            pl.BlockSpec((1, 2, 1, n), lambda i: (i, 0, 0, 0)),
        ],
        out_shape=[
            jax.ShapeDtypeStruct((g, chunk, 2, n, p), F32),
            jax.ShapeDtypeStruct((g, chunk, 2, p, n), F32),
            jax.ShapeDtypeStruct((g, chunk, p, p), F32),
            jax.ShapeDtypeStruct((g, 2, 1, n), F32),
        ],
        compiler_params=_cparams(("arbitrary",)),
        name="s5_discretise",
    )(a_re.reshape(g, n, 1), a_im.reshape(g, n, 1), a_re.reshape(g, 1, n), a_im.reshape(g, 1, n),
      log_dt.reshape(g, 1, 1), b_re, b_im, c_re, c_im)


def _s5_layout_weights(winp, coutp, kl, lam):
    g, t, _, n, p = winp.shape
    tp = t * p
    win = jnp.flip(winp, axis=1).transpose(0, 1, 4, 2, 3).reshape(g, tp, 2, n)
    wout = coutp.transpose(0, 2, 4, 1, 3).reshape(g, 2, n, tp)
    s_idx = jnp.arange(t)[:, None]
    t_idx = jnp.arange(t)[None, :]
    d = t_idx - s_idx
    kk = kl[:, jnp.clip(d, 0, t - 1)]
    kk = jnp.where((d >= 0)[None, :, :, None, None], kk, 0.0)
    mt = kk.transpose(0, 1, 4, 2, 3).reshape(g, tp, tp)
    half = jnp.arange(g) % 2
    sel = (half[:, None] == jnp.arange(2)[None, :]).astype(F32)
    win2 = (win[:, :, :, None, :] * sel[:, None, None, :, None]).reshape(g, tp, 2, 2 * n)
    win2 = win2.transpose(0, 2, 1, 3)
    wout2 = (wout[:, :, None, :, :] * sel[:, None, :, None, None]).reshape(g, 2, 2 * n, tp)
    lam2 = lam.reshape(g // 2, 2, 2, n).transpose(0, 2, 1, 3).reshape(g // 2, 2, 1, 2 * n)
    return win2, wout2, mt, lam2


def _s5_kernel(u_ref, mt_ref, win_ref, wout_ref, lam_ref, dsk_ref, y_ref,
               xr_ref, xi_ref, er_ref, ei_ref, *, bsz, nchunks):
    u0 = u_ref[0]
    u1 = u_ref[1]
    xr_ref[...] = _dot(u0, win_ref[0, 0]) + _dot(u1, win_ref[1, 0])
    xi_ref[...] = _dot(u0, win_ref[0, 1]) + _dot(u1, win_ref[1, 1])
    ar = lam_ref[0, 0]
    ai = lam_ref[0, 1]

    def step(c, carry):
        new = []
        for b in range(bsz):
            er, ei = carry[b]
            row = b * nchunks + c
            er_ref[pl.ds(row, 1), :] = er
            ei_ref[pl.ds(row, 1), :] = ei
            xr = xr_ref[pl.ds(row, 1), :]
            xi = xi_ref[pl.ds(row, 1), :]
            new.append((er * ar - ei * ai + xr, ei * ar + er * ai + xi))
        return tuple(new)

    zero = jnp.zeros_like(ar)
    lax.fori_loop(0, nchunks, step, tuple((zero, zero) for _ in range(bsz)), unroll=8)

    er = er_ref[...]
    ei = ei_ref[...]
    for j, u in enumerate((u0, u1)):
        y = _dot(u, mt_ref[j]) + _dot(er, wout_ref[j, 0]) + _dot(ei, wout_ref[j, 1])
        y_ref[j] = y + u * dsk_ref[j]


def _s5_mixer(u, mt, win2, wout2, lam2, dsk, bsz, nchunks):
    g, r, tp = u.shape
    n2 = win2.shape[-1]
    g3 = lambda i: (i, 0, 0)
    g4 = lambda i: (i, 0, 0, 0)
    return pl.pallas_call(
        functools.partial(_s5_kernel, bsz=bsz, nchunks=nchunks),
        grid=(g // 2,),
        in_specs=[
            pl.BlockSpec((2, r, tp), g3),
            pl.BlockSpec((2, tp, tp), g3),
            pl.BlockSpec((2, 2, tp, n2), g4),
            pl.BlockSpec((2, 2, n2, tp), g4),
            pl.BlockSpec((1, 2, 1, n2), g4),
            pl.BlockSpec((2, 1, tp), g3),
        ],
        out_specs=pl.BlockSpec((2, r, tp), g3),
        out_shape=jax.ShapeDtypeStruct((g, r, tp), F32),
        scratch_shapes=[pltpu.VMEM((r, n2), F32) for _ in range(4)],
        compiler_params=_cparams(("arbitrary",)),
        name="s5_mixer",
    )(u, mt, win2, wout2, lam2, dsk)


def _float_key(s):
    bits = pltpu.bitcast(s, I32)
    return jnp.where(bits < 0, INT_MIN - bits, bits)


def _dsa_kernel(q_ref, qi_ref, wi_ref, k_ref, v_ref, ki_ref, o_ref,
                key_ref, m_ref, acc_ref, *, tq, tk, top_k, idx_scale, pos_bits):
    i = pl.program_id(1)
    n_tiles = (i * tq + tq + tk - 1) // tk
    q_pos = i * tq + lax.broadcasted_iota(I32, (tq, tk), 0)
    lane = lax.broadcasted_iota(I32, (tq, tk), 1)

    qi = qi_ref[0]
    wi = wi_ref[0] * idx_scale
    qi_h = [qi[:, h * IDX_DIM:(h + 1) * IDX_DIM] for h in range(IDX_HEADS)]
    wi_h = [wi[:, h:h + 1] for h in range(IDX_HEADS)]

    def score_tile(t, carry):
        start = pl.multiple_of(t * tk, tk)
        kit = ki_ref[0, pl.ds(start, tk), :]
        sc = jnp.zeros((tq, tk), F32)
        for h in range(IDX_HEADS):
            sc = sc + wi_h[h] * jnp.maximum(_dot_nt(qi_h[h], kit), 0.0)
        key = jnp.where(start + lane <= q_pos, _float_key(sc), INT_MIN)
        key_ref[:, pl.ds(start, tk)] = key
        return carry

    lax.fori_loop(0, n_tiles, score_tile, 0)

    def count_ge(cand):
        def body(t, cnt):
            start = pl.multiple_of(t * tk, tk)
            return cnt + (key_ref[:, pl.ds(start, tk)] >= cand).astype(I32)
        cnt = lax.fori_loop(0, n_tiles, body, jnp.zeros((tq, tk), I32))
        return jnp.sum(cnt, axis=1, keepdims=True)

    def bisect(p, prefix):
        cand = prefix + jnp.left_shift(jnp.int32(1), 31 - p)
        return jnp.where(count_ge(cand) >= top_k, cand, prefix)

    tau = lax.fori_loop(0, 32, bisect, jnp.full((tq, 1), INT_MIN, I32))

    def count_gt_eq(t, carry):
        gt, eq = carry
        start = pl.multiple_of(t * tk, tk)
        key = key_ref[:, pl.ds(start, tk)]
        return gt + (key > tau).astype(I32), eq + (key == tau).astype(I32)

    zeros = jnp.zeros((tq, tk), I32)
    gt, eq = lax.fori_loop(0, n_tiles, count_gt_eq, (zeros, zeros))
    n_gt = jnp.sum(gt, axis=1, keepdims=True)
    n_eq = jnp.sum(eq, axis=1, keepdims=True)
    need = top_k - n_gt
    tied = jnp.logical_and(n_eq > need, tau > INT_MIN)

    @pl.when(jnp.max(tied.astype(I32)) > 0)
    def _():
        def count_eq_below(pos):
            def body(t, cnt):
                start = pl.multiple_of(t * tk, tk)
                key = key_ref[:, pl.ds(start, tk)]
                hit = jnp.logical_and(key == tau, start + lane < pos)
                return cnt + hit.astype(I32)
            cnt = lax.fori_loop(0, n_tiles, body, zeros)
            return jnp.sum(cnt, axis=1, keepdims=True)

        def bisect_pos(p, last):
            cand = last + jnp.left_shift(jnp.int32(1), pos_bits - 1 - p)
            return jnp.where(count_eq_below(cand) < need, cand, last)

        last = lax.fori_loop(0, pos_bits, bisect_pos, jnp.zeros((tq, 1), I32))

        def demote(t, carry):
            start = pl.multiple_of(t * tk, tk)
            key = key_ref[:, pl.ds(start, tk)]
            drop = jnp.logical_and(jnp.logical_and(key == tau, start + lane > last), tied)
            key_ref[:, pl.ds(start, tk)] = jnp.where(drop, tau - 1, key)
            return carry

        lax.fori_loop(0, n_tiles, demote, 0)

    thr = jnp.maximum(tau, INT_MIN + 1)

    q = q_ref[0]
    q_h = [q[:, h * HEAD_DIM:(h + 1) * HEAD_DIM] for h in range(ATTN_HEADS)]
    m_ref[...] = jnp.full(m_ref.shape, NEG_BIG, F32)
    acc_ref[...] = jnp.zeros(acc_ref.shape, F32)

    def attend(t, carry):
        start = pl.multiple_of(t * tk, tk)
        bias = jnp.where(key_ref[:, pl.ds(start, tk)] >= thr, 0.0, NEG_BIG)
        kt = k_ref[0, pl.ds(start, tk), :]
        vt = v_ref[0, pl.ds(start, tk), :]
        for h in range(ATTN_HEADS):
            s = _dot_nt(q_h[h], kt) + bias
            m_old = m_ref[h]
            m_new = jnp.maximum(m_old, jnp.max(s, axis=1, keepdims=True))
            p = jnp.exp(s - m_new)
            acc_ref[h] = jnp.exp(m_old - m_new) * acc_ref[h] + _dot(p.astype(BF16), vt)
            m_ref[h] = m_new
        return carry

    lax.fori_loop(0, n_tiles, attend, 0)

    outs = []
    for h in range(ATTN_HEADS):
        a = acc_ref[h]
        outs.append(a[:, :HEAD_DIM] / a[:, HEAD_DIM:HEAD_DIM + 1])
    o_ref[0] = jnp.concatenate(outs, axis=1)


def _dsa_mixer(q, qi, wi, k, vext, ki, top_k, idx_scale, tq, tk):
    bsz, seq, aw = q.shape
    nq = seq // tq
    blk = lambda b, i: (b, i, 0)
    full = lambda b, i: (b, 0, 0)
    kern = functools.partial(_dsa_kernel, tq=tq, tk=tk, top_k=top_k, idx_scale=idx_scale,
                             pos_bits=max(1, (seq - 1).bit_length()))
    return pl.pallas_call(
        kern,
        grid=(bsz, nq),
        in_specs=[
            pl.BlockSpec((1, tq, aw), blk),
            pl.BlockSpec((1, tq, qi.shape[-1]), blk),
            pl.BlockSpec((1, tq, wi.shape[-1]), blk),
            pl.BlockSpec((1, seq, k.shape[-1]), full),
            pl.BlockSpec((1, seq, vext.shape[-1]), full),
            pl.BlockSpec((1, seq, ki.shape[-1]), full),
        ],
        out_specs=pl.BlockSpec((1, tq, aw), blk),
        out_shape=jax.ShapeDtypeStruct((bsz, seq, aw), F32),
        scratch_shapes=[
            pltpu.VMEM((tq, seq), I32),
            pltpu.VMEM((ATTN_HEADS, tq, 1), F32),
            pltpu.VMEM((ATTN_HEADS, tq, LANES), F32),
        ],
        compiler_params=_cparams(("arbitrary", "arbitrary")),
        name="dsa_mixer",
    )(q, qi, wi, k, vext, ki)


def _gelu_tanh(x):
    return 0.5 * x * (1.0 + jnp.tanh(math.sqrt(2.0 / math.pi) * (x + 0.044715 * (x * x * x))))


def _out_kernel(x_ref, g_ref, mod_ref, ys_ref, ob_ref, wz_ref, wglu_ref, bglu_ref,
                wao_ref, wbo_ref, wout_ref, fg_ref, out_ref, *, sw, aw, final):
    x = x_ref[...]
    mod = mod_ref[0]
    h = _norm_mod(x, g_ref[...], mod).astype(BF16)
    z = _dot(h, wz_ref[...])
    d = x.shape[-1]
    za = z[:, :sw]
    zb = z[:, sw:sw + aw]
    ga = z[:, sw + aw:sw + aw + d]
    gb = z[:, sw + aw + d:]

    ya = _gelu_tanh(ys_ref[...])
    glu = _dot(ya.astype(BF16), wglu_ref[...]) + bglu_ref[...]
    ya = glu[:, :sw] * _sigmoid(glu[:, sw:])
    ya = _dot((ya * (za * _sigmoid(za))).astype(BF16), wao_ref[...])
    yb = _dot((ob_ref[...] * (zb * _sigmoid(zb))).astype(BF16), wbo_ref[...])
    merged = _sigmoid(ga) * ya + _sigmoid(gb) * yb
    y = x + mod[2:3, :] * _dot(merged.astype(BF16), wout_ref[...])
    if final:
        ms = jnp.mean(y * y, axis=-1, keepdims=True)
        y = y * lax.rsqrt(ms + RMS_EPS) * fg_ref[...]
    out_ref[...] = y


def _merge_out(x2, g, mod, ys, ob, wz, wglu, bglu, wao, wbo, wout, fg, rows_per_batch, tm, final):
    r, d = x2.shape
    sw = ys.shape[-1]
    aw = ob.shape[-1]
    tiles_per_batch = rows_per_batch // tm
    row = lambda i: (i, 0)
    const = lambda i: (0, 0)
    return pl.pallas_call(
        functools.partial(_out_kernel, sw=sw, aw=aw, final=final),
        grid=(r // tm,),
        in_specs=[
            pl.BlockSpec((tm, d), row),
            pl.BlockSpec((1, d), const),
            pl.BlockSpec((1, 3, d), lambda i: (i // tiles_per_batch, 0, 0)),
            pl.BlockSpec((tm, sw), row),
            pl.BlockSpec((tm, aw), row),
            pl.BlockSpec(wz.shape, const),
            pl.BlockSpec(wglu.shape, const),
            pl.BlockSpec((1, wglu.shape[-1]), const),
            pl.BlockSpec(wao.shape, const),
            pl.BlockSpec(wbo.shape, const),
            pl.BlockSpec(wout.shape, const),
            pl.BlockSpec((1, d), const),
        ],
        out_specs=pl.BlockSpec((tm, d), row),
        out_shape=jax.ShapeDtypeStruct((r, d), F32),
        compiler_params=_cparams(("arbitrary",)),
        name="merge_out",
    )(x2, g, mod, ys, ob, wz, wglu, bglu, wao, wbo, wout, fg)


def _split_points(d):
    sw = d // 2
    aw = ATTN_HEADS * HEAD_DIM
    iw = IDX_HEADS * IDX_DIM
    widths = (sw, sw, aw, HEAD_DIM, HEAD_DIM, aw, iw, IDX_DIM, IDX_HEADS, d, d)
    offs = [0]
    for w in widths:
        offs.append(offs[-1] + w)
    return widths, offs


def kernel(x, c, norm_g, w_mod, b_mod, w_in, a_re, a_im, b_re, b_im, c_re, c_im, d_skip, log_dt,
           w_glu, b_glu, w_a_o, w_b_o, w_out, final_g):
    bsz, seq, d = x.shape
    depth = w_in.shape[0]
    r = bsz * seq
    sw = d // 2
    aw = ATTN_HEADS * HEAD_DIM
    groups = sw // SSM_GROUP
    chunk = SSM_CHUNK
    nchunks = seq // chunk
    top_k = min(TOPK_MAX, seq // 4)
    idx_scale = (IDX_HEADS * IDX_DIM) ** -0.5
    tm = min(512, seq)
    tq = min(128, seq)
    tk = min(256, seq)
    _, offs = _split_points(d)
    (o_xa, o_za, o_q, o_k, o_v, o_zb, o_qi, o_ki, o_wi, o_ga, o_gb, o_end) = offs

    mod_all = _modulation(c, w_mod, b_mod).reshape(depth, bsz, 3, d)
    x2 = x.reshape(r, d)
    for l in range(depth):
        w = w_in[l]
        small = jnp.concatenate([w[:, o_k:o_zb], w[:, o_ki:o_ga]], axis=1)
        pad = (-small.shape[1]) % LANES
        w_cat = jnp.concatenate([w[:, o_xa:o_za], w[:, o_q:o_k], w[:, o_qi:o_ki],
                                 jnp.pad(small, ((0, 0), (0, pad)))], axis=1).astype(BF16)
        widths = (sw, aw, IDX_HEADS * IDX_DIM, small.shape[1] + pad)
        g = norm_g[l].reshape(1, d)
        mod = mod_all[l]
        xa, q, qi, sm = _input_proj(x2, g, mod, w_cat, widths, seq, tm)

        winp, coutp, kl, lam = _s5_weights(a_re[l], a_im[l], b_re[l], b_im[l], c_re[l], c_im[l],
                                           log_dt[l], chunk)
        win2, wout2, mt, lam2 = _s5_layout_weights(winp, coutp, kl, lam)
        u = xa.reshape(bsz, nchunks, chunk, groups, SSM_GROUP).transpose(3, 0, 1, 2, 4)
        u = u.reshape(groups, bsz * nchunks, chunk * SSM_GROUP)
        dsk = jnp.tile(d_skip[l].reshape(groups, 1, SSM_GROUP), (1, chunk, 1)).reshape(
            groups, 1, chunk * SSM_GROUP)
        ys = _s5_mixer(u, mt, win2, wout2, lam2, dsk, bsz, nchunks)
        ys = ys.reshape(groups, bsz, nchunks, chunk, SSM_GROUP).transpose(1, 2, 3, 0, 4)
        ys = ys.reshape(r, sw)

        kk = sm[:, 0:HEAD_DIM].astype(BF16).reshape(bsz, seq, HEAD_DIM)
        vv = sm[:, HEAD_DIM:2 * HEAD_DIM]
        vext = jnp.concatenate([vv, jnp.ones((r, 1), F32), jnp.zeros((r, LANES - HEAD_DIM - 1), F32)],
                               axis=1).astype(BF16).reshape(bsz, seq, LANES)
        ki = sm[:, 2 * HEAD_DIM:2 * HEAD_DIM + IDX_DIM].astype(BF16).reshape(bsz, seq, IDX_DIM)
        wi = sm[:, 2 * HEAD_DIM + IDX_DIM:2 * HEAD_DIM + IDX_DIM + IDX_HEADS].reshape(bsz, seq, IDX_HEADS)
        ob = _dsa_mixer(q.reshape(bsz, seq, aw), qi.reshape(bsz, seq, -1), wi, kk, vext, ki,
                        top_k, idx_scale, tq, tk).reshape(r, aw)

        wz = jnp.concatenate([w[:, o_za:o_q], w[:, o_zb:o_qi], w[:, o_ga:o_end]], axis=1).astype(BF16)
        x2 = _merge_out(x2, g, mod, ys, ob, wz, w_glu[l].astype(BF16), b_glu[l].reshape(1, -1),
                        w_a_o[l].astype(BF16), w_b_o[l].astype(BF16), w_out[l].astype(BF16),
                        final_g.reshape(1, d), seq, tm, final=(l == depth - 1))
    return x2.reshape(bsz, seq, d)
```

```python
import functools
import math

import jax
import jax.numpy as jnp
from jax import lax
from jax.experimental import pallas as pl
from jax.experimental.pallas import tpu as pltpu

F32 = jnp.float32
BF16 = jnp.bfloat16
I32 = jnp.int32

RMS_EPS = 1e-6
SSM_GROUP = 16
SSM_STATE = 64
SSM_CHUNK = 8
SSM_ROWS = 2048
ATTN_HEADS = 8
HEAD_DIM = 64
IDX_HEADS = 8
IDX_DIM = 64
TOPK_MAX = 256

V7X_VMEM_LIMIT = 56 * 1024 * 1024
LANES = 128

INT_MIN = -2 ** 31
NEG_BIG = -1e30
LOG2E = math.log2(math.e)


def _cparams(sem):
    return pltpu.CompilerParams(dimension_semantics=sem, vmem_limit_bytes=V7X_VMEM_LIMIT)


def _sigmoid(x):
    return 1.0 / (1.0 + jnp.exp(-x))


def _dot(a, b, **kw):
    return jnp.dot(a, b, preferred_element_type=F32, **kw)


def _dot_nt(a, b):
    return lax.dot_general(a, b, (((1,), (1,)), ((), ())), preferred_element_type=F32)


def _mod_kernel(c_ref, w_ref, b_ref, o_ref):
    c = c_ref[...]
    sc = c * _sigmoid(c)
    o_ref[0] = _dot(sc, w_ref[0], precision=lax.Precision.HIGHEST) + b_ref[0]


def _modulation(c, w_mod, b_mod):
    depth, d, d3 = w_mod.shape
    bsz = c.shape[0]
    nj = d3 // d
    return pl.pallas_call(
        _mod_kernel,
        grid=(depth, nj),
        in_specs=[
            pl.BlockSpec((bsz, d), lambda l, j: (0, 0)),
            pl.BlockSpec((1, d, d), lambda l, j: (l, 0, j)),
            pl.BlockSpec((1, 1, d), lambda l, j: (l, 0, j)),
        ],
        out_specs=pl.BlockSpec((1, bsz, d), lambda l, j: (l, 0, j)),
        out_shape=jax.ShapeDtypeStruct((depth, bsz, d3), F32),
        compiler_params=_cparams(("arbitrary", "arbitrary")),
        name="adaln_mod",
    )(c, w_mod, b_mod.reshape(depth, 1, d3))


def _norm_mod(x, g, mod):
    ms = jnp.mean(x * x, axis=-1, keepdims=True)
    y = x * lax.rsqrt(ms + RMS_EPS) * g
    return y * (1.0 + mod[1:2, :]) + mod[0:1, :]


def _proj_kernel(x_ref, g_ref, mod_ref, w_ref, xa_ref, q_ref, qi_ref, sm_ref, *, widths):
    h = _norm_mod(x_ref[...], g_ref[...], mod_ref[0]).astype(BF16)
    p = _dot(h, w_ref[...])
    w_xa, w_q, w_qi, _ = widths
    o = 0
    xa_ref[...] = p[:, o:o + w_xa]
    o += w_xa
    for h in range(ATTN_HEADS):
        lo = o + h * HEAD_DIM
        q_ref[0, h] = (p[:, lo:lo + HEAD_DIM] * (HEAD_DIM ** -0.5 * LOG2E)).astype(BF16)
    o += w_q
    for h in range(IDX_HEADS):
        lo = o + h * IDX_DIM
        qi_ref[0, h] = p[:, lo:lo + IDX_DIM].astype(BF16)
    o += w_qi
    sm_ref[...] = p[:, o:]


def _input_proj(x2, g, mod, w_cat, widths, rows_per_batch, tm):
    r, d = x2.shape
    w_xa, w_q, w_qi, w_sm = widths
    tiles_per_batch = rows_per_batch // tm
    bsz = r // rows_per_batch
    row = lambda i: (i, 0)
    heads = lambda i: (i // tiles_per_batch, 0, i % tiles_per_batch, 0)
    return pl.pallas_call(
        functools.partial(_proj_kernel, widths=widths),
        grid=(r // tm,),
        in_specs=[
            pl.BlockSpec((tm, d), row),
            pl.BlockSpec((1, d), lambda i: (0, 0)),
            pl.BlockSpec((1, 3, d), lambda i: (i // tiles_per_batch, 0, 0)),
            pl.BlockSpec(w_cat.shape, lambda i: (0, 0)),
        ],
        out_specs=[
            pl.BlockSpec((tm, w_xa), row),
            pl.BlockSpec((1, ATTN_HEADS, tm, HEAD_DIM), heads),
            pl.BlockSpec((1, IDX_HEADS, tm, IDX_DIM), heads),
            pl.BlockSpec((tm, w_sm), row),
        ],
        out_shape=[
            jax.ShapeDtypeStruct((r, w_xa), F32),
            jax.ShapeDtypeStruct((bsz, ATTN_HEADS, rows_per_batch, HEAD_DIM), BF16),
            jax.ShapeDtypeStruct((bsz, IDX_HEADS, rows_per_batch, IDX_DIM), BF16),
            jax.ShapeDtypeStruct((r, w_sm), F32),
        ],
        compiler_params=_cparams(("arbitrary",)),
        name="input_proj",
    )(x2, g, mod, w_cat)


def _s5w_kernel(arc_ref, aic_ref, arr_ref, air_ref, ldt_ref, br_ref, bi_ref, cr_ref, ci_ref,
                winp_ref, coutp_ref, kl_ref, lam_ref, *, chunk):
    dt = jnp.exp(ldt_ref[0])

    def discretise(ar, ai):
        mag = jnp.exp(ar * dt)
        return mag * jnp.cos(ai * dt), mag * jnp.sin(ai * dt)

    ar, ai = arc_ref[0], aic_ref[0]
    lr, li = discretise(ar, ai)
    den = ar * ar + ai * ai
    nr = lr - 1.0
    zr = (nr * ar + li * ai) / den
    zi = (li * ar - nr * ai) / den
    br, bi = br_ref[0], bi_ref[0]
    bbr = zr * br - zi * bi
    bbi = zr * bi + zi * br

    rr, ri = discretise(arr_ref[0], air_ref[0])
    cr, ci = cr_ref[0], ci_ref[0]

    pr, pi = jnp.ones_like(lr), jnp.zeros_like(li)
    qr, qi = jnp.ones_like(rr), jnp.zeros_like(ri)
    hi = lax.Precision.HIGHEST
    for k in range(chunk):
        winp_ref[0, k, 0] = pr * bbr - pi * bbi
        winp_ref[0, k, 1] = pr * bbi + pi * bbr
        ckr = cr * qr - ci * qi
        cki = cr * qi + ci * qr
        kl_ref[0, k] = _dot(ckr, bbr, precision=hi) - _dot(cki, bbi, precision=hi)
        pr, pi = pr * lr - pi * li, pr * li + pi * lr
        qr, qi = qr * rr - qi * ri, qr * ri + qi * rr
        coutp_ref[0, k, 0] = cr * qr - ci * qi
        coutp_ref[0, k, 1] = -(cr * qi + ci * qr)
    lam_ref[0, 0] = qr
    lam_ref[0, 1] = qi


def _s5_weights(a_re, a_im, b_re, b_im, c_re, c_im, log_dt, chunk):
    g, n = a_re.shape
    p = b_re.shape[-1]
    g3 = lambda i: (i, 0, 0)
    col = pl.BlockSpec((1, n, 1), g3)
    rowb = pl.BlockSpec((1, 1, n), g3)
    return pl.pallas_call(
        functools.partial(_s5w_kernel, chunk=chunk),
        grid=(g,),
        in_specs=[col, col, rowb, rowb, pl.BlockSpec((1, 1, 1), g3),
                  pl.BlockSpec((1, n, p), g3), pl.BlockSpec((1, n, p), g3),
                  pl.BlockSpec((1, p, n), g3), pl.BlockSpec((1, p, n), g3)],
        out_specs=[
            pl.BlockSpec((1, chunk, 2, n, p), lambda i: (i, 0, 0, 0, 0)),
            pl.BlockSpec((1, chunk, 2, p, n), lambda i: (i, 0, 0, 0, 0)),
            pl.BlockSpec((1, chunk, p, p), lambda i: (i, 0, 0, 0)),
            pl.BlockSpec((1, 2, 1, n), lambda i: (i, 0, 0, 0)),
        ],
        out_shape=[
            jax.ShapeDtypeStruct((g, chunk, 2, n, p), F32),
            jax.ShapeDtypeStruct((g, chunk, 2, p, n), F32),
            jax.ShapeDtypeStruct((g, chunk, p, p), F32),
            jax.ShapeDtypeStruct((g, 2, 1, n), F32),
        ],
        compiler_params=_cparams(("arbitrary",)),
        name="s5_discretise",
    )(a_re.reshape(g, n, 1), a_im.reshape(g, n, 1), a_re.reshape(g, 1, n), a_im.reshape(g, 1, n),
      log_dt.reshape(g, 1, 1), b_re, b_im, c_re, c_im)


def _s5_layout_weights(winp, coutp, kl, lam):
    g, t, _, n, p = winp.shape
    gl = LANES // p
    nl = g // gl
    eye = jnp.eye(gl, dtype=F32)
    a = jnp.flip(winp, axis=1).reshape(nl, gl, t, 2, n, p)
    win = jnp.einsum('lgsrnq,gh->lsgqrhn', a, eye).reshape(nl, t * LANES, 2 * gl * n)
    s_idx = jnp.arange(t)[:, None]
    t_idx = jnp.arange(t)[None, :]
    d = t_idx - s_idx
    kk = kl[:, jnp.clip(d, 0, t - 1)]
    kk = jnp.where((d >= 0)[None, :, :, None, None], kk, 0.0).reshape(nl, gl, t, t, p, p)
    mt = jnp.einsum('lgstpq,gh->lsgqthp', kk, eye).reshape(nl, t * LANES, t * LANES)
    c = coutp.reshape(nl, gl, t, 2, p, n)
    wout = jnp.einsum('lgtrpn,gh->lrhntgp', c, eye).reshape(nl, 2, gl * n, t * LANES)
    lam2 = lam.reshape(nl, gl, 2, n).transpose(0, 2, 1, 3).reshape(nl, 2, 1, gl * n)
    return win.astype(BF16), mt.astype(BF16), wout.astype(BF16), lam2


def _s5_kernel(x_ref, win_ref, mt_ref, wout_ref, lam_ref, dsk_ref, y_ref,
               xr_ref, xi_ref, er_ref, ei_ref, sr_ref, si_ref, *, chunk, nchunks):
    ns = sr_ref.shape[-1]

    @pl.when(pl.program_id(2) == 0)
    def _():
        sr_ref[...] = jnp.zeros(sr_ref.shape, F32)
        si_ref[...] = jnp.zeros(si_ref.shape, F32)

    u = jnp.concatenate([x_ref[pl.ds(s, nchunks, stride=chunk), :] for s in range(chunk)], axis=1)
    ub = u.astype(BF16)
    x = _dot(ub, win_ref[0])
    xr_ref[...] = x[:, :ns]
    xi_ref[...] = x[:, ns:]
    ar = lam_ref[0, 0]
    ai = lam_ref[0, 1]

    def step(c, carry):
        er, ei = carry
        er_ref[pl.ds(c, 1), :] = er
        ei_ref[pl.ds(c, 1), :] = ei
        return (er * ar - ei * ai + xr_ref[pl.ds(c, 1), :],
                ei * ar + er * ai + xi_ref[pl.ds(c, 1), :])

    er, ei = lax.fori_loop(0, nchunks, step, (sr_ref[...], si_ref[...]), unroll=8)
    sr_ref[...] = er
    si_ref[...] = ei

    y = (_dot(ub, mt_ref[0]) + _dot(er_ref[...].astype(BF16), wout_ref[0, 0])
         + _dot(ei_ref[...].astype(BF16), wout_ref[0, 1]) + u * dsk_ref[0])
    for s in range(chunk):
        y_ref[pl.ds(s, nchunks, stride=chunk), :] = y[:, s * LANES:(s + 1) * LANES]


def _s5_mixer(xa, win, mt, wout, lam2, dsk, rows_per_batch, chunk, rows):
    r, sw = xa.shape
    nl = sw // LANES
    bsz = r // rows_per_batch
    nb = rows_per_batch // rows
    nchunks = rows // chunk
    ns = lam2.shape[-1]
    tile = lambda l, b, j: (l, 0, 0)
    tile4 = lambda l, b, j: (l, 0, 0, 0)
    data = lambda l, b, j: (b * nb + j, l)
    return pl.pallas_call(
        functools.partial(_s5_kernel, chunk=chunk, nchunks=nchunks),
        grid=(nl, bsz, nb),
        in_specs=[
            pl.BlockSpec((rows, LANES), data),
            pl.BlockSpec((1,) + win.shape[1:], tile),
            pl.BlockSpec((1,) + mt.shape[1:], tile),
            pl.BlockSpec((1,) + wout.shape[1:], tile4),
            pl.BlockSpec((1,) + lam2.shape[1:], tile4),
            pl.BlockSpec((1,) + dsk.shape[1:], tile),
        ],
        out_specs=pl.BlockSpec((rows, LANES), data),
        out_shape=jax.ShapeDtypeStruct((r, sw), F32),
        scratch_shapes=[pltpu.VMEM((nchunks, ns), F32) for _ in range(4)]
        + [pltpu.VMEM((1, ns), F32) for _ in range(2)],
        compiler_params=_cparams(("arbitrary", "arbitrary", "arbitrary")),
        name="s5_mixer",
    )(xa, win, mt, wout, lam2, dsk)


def _float_key(s):
    bits = pltpu.bitcast(s, I32)
    return jnp.where(bits < 0, INT_MIN - bits, bits)


def _row_any(flag):
    return jnp.max(flag.astype(I32))


def _dsa_kernel(q_ref, qi_ref, wi_ref, k_ref, v_ref, ki_ref, o_ref,
                key_ref, wb_ref, m_ref, acc_ref, *, tq, tk, top_k, idx_scale, pos_bits):
    i = pl.program_id(1)
    n_tiles = (i * tq + tq + tk - 1) // tk
    half = tk // 2
    q_pos = i * tq + lax.broadcasted_iota(I32, (tq, half), 0)
    lane_h = lax.broadcasted_iota(I32, (tq, half), 1)
    lane = lax.broadcasted_iota(I32, (tq, tk), 1)
    nck = tk // LANES

    wi = wi_ref[0] * idx_scale
    for h in range(IDX_HEADS):
        wb_ref[h * tq:(h + 1) * tq, :] = jnp.broadcast_to(wi[:, h:h + 1], (tq, LANES))

    def score_half(start, a0, a1):
        y = _dot_nt(qi_ref[0].reshape(IDX_HEADS * tq, IDX_DIM), ki_ref[0, pl.ds(start, half), :])
        wb = wb_ref[...]
        cols = []
        for c in range(half // LANES):
            yc = jnp.maximum(y[:, c * LANES:(c + 1) * LANES], 0.0) * wb
            sc = yc[0:tq]
            for h in range(1, IDX_HEADS):
                sc = sc + yc[h * tq:(h + 1) * tq]
            cols.append(sc)
        sc = jnp.concatenate(cols, axis=1)
        key = jnp.where(start + lane_h <= q_pos, _float_key(sc), INT_MIN)
        key_ref[:, pl.ds(start, half)] = key
        for c in range(half // LANES):
            kc = key[:, c * LANES:(c + 1) * LANES]
            if c % 2 == 0:
                a0 = jnp.maximum(a0, kc)
            else:
                a1 = jnp.maximum(a1, kc)
        return a0, a1

    def score_tile(t, carry):
        start = pl.multiple_of(t * tk, tk)
        a0, a1 = score_half(start, *carry)
        return score_half(start + half, a0, a1)

    lowest = jnp.full((tq, LANES), INT_MIN, I32)
    a0, a1 = lax.fori_loop(0, n_tiles, score_tile, (lowest, lowest))

    lo0 = jnp.minimum(jnp.min(a0, axis=1, keepdims=True), jnp.min(a1, axis=1, keepdims=True))
    hi0 = jnp.maximum(jnp.max(a0, axis=1, keepdims=True), jnp.max(a1, axis=1, keepdims=True)) + 1

    def count_ge(cand):
        def body(t, cnt):
            start = pl.multiple_of(t * tk, tk)
            for c in range(nck):
                kc = key_ref[:, pl.ds(start + c * LANES, LANES)]
                cnt = cnt + (kc >= cand).astype(I32)
            return cnt
        cnt = lax.fori_loop(0, n_tiles, body, jnp.zeros((tq, LANES), I32))
        return jnp.sum(cnt, axis=1, keepdims=True)

    def bisect_cond(st):
        return st[0] > 0

    def bisect_body(st):
        _, it, lo, hi, thr, done = st
        active = jnp.logical_and(done == 0, lo + 1 != hi)
        go = jnp.where(it < 40, _row_any(active), 0)
        mid = (lo >> 1) + (hi >> 1) + (lo & hi & 1)
        cnt = count_ge(mid)
        ge = cnt >= top_k
        hit = jnp.logical_and(jnp.logical_and(cnt == top_k, done == 0), mid > INT_MIN)
        lo = jnp.where(ge, mid, lo)
        hi = jnp.where(ge, hi, mid)
        thr = jnp.where(hit, mid, thr)
        done = jnp.where(hit, 1, done)
        return go, it + 1, lo, hi, thr, done

    zcol = jnp.zeros((tq, 1), I32)
    _, _, tau, _, thr, done = lax.while_loop(
        bisect_cond, bisect_body, (jnp.int32(1), jnp.int32(0), lo0, hi0, zcol, zcol))

    open_rows = jnp.logical_and(done == 0, tau > INT_MIN)
    thr = jnp.where(done == 1, thr, jnp.maximum(tau, INT_MIN + 1))

    @pl.when(_row_any(open_rows) > 0)
    def _():
        zeros = jnp.zeros((tq, LANES), I32)

        def count_gt_eq(t, carry):
            gt, eq = carry
            start = pl.multiple_of(t * tk, tk)
            for c in range(nck):
                kc = key_ref[:, pl.ds(start + c * LANES, LANES)]
                gt = gt + (kc > tau).astype(I32)
                eq = eq + (kc == tau).astype(I32)
            return gt, eq

        gt, eq = lax.fori_loop(0, n_tiles, count_gt_eq, (zeros, zeros))
        need = top_k - jnp.sum(gt, axis=1, keepdims=True)
        tied = jnp.logical_and(open_rows, jnp.sum(eq, axis=1, keepdims=True) > need)

        def count_eq_below(pos):
            def body(t, cnt):
                start = pl.multiple_of(t * tk, tk)
                kt = key_ref[:, pl.ds(start, tk)]
                hit = jnp.logical_and(kt == tau, start + lane < pos).astype(I32)
                for c in range(nck):
                    cnt = cnt + hit[:, c * LANES:(c + 1) * LANES]
                return cnt
            return jnp.sum(lax.fori_loop(0, n_tiles, body, zeros), axis=1, keepdims=True)

        def bisect_pos(p, last):
            cand = last + jnp.left_shift(jnp.int32(1), pos_bits - 1 - p)
            return jnp.where(count_eq_below(cand) < need, cand, last)

        last = lax.fori_loop(0, pos_bits, bisect_pos, zcol)

        def demote(t, carry):
            start = pl.multiple_of(t * tk, tk)
            kt = key_ref[:, pl.ds(start, tk)]
            drop = jnp.logical_and(jnp.logical_and(kt == tau, start + lane > last), tied)
            key_ref[:, pl.ds(start, tk)] = jnp.where(drop, tau - 1, kt)
            return carry

        lax.fori_loop(0, n_tiles, demote, 0)

    m_ref[...] = jnp.full(m_ref.shape, NEG_BIG, F32)
    acc_ref[...] = jnp.zeros(acc_ref.shape, F32)

    def attend(t, carry):
        start = pl.multiple_of(t * tk, tk)
        bias = jnp.where(key_ref[:, pl.ds(start, tk)] >= thr, 0.0, NEG_BIG)
        s = _dot_nt(q_ref[0].reshape(ATTN_HEADS * tq, HEAD_DIM), k_ref[0, pl.ds(start, tk), :])
        s = (s.reshape(ATTN_HEADS, tq, tk) + bias[None]).reshape(ATTN_HEADS * tq, tk)
        m_old = m_ref[...]
        m_new = jnp.maximum(m_old, jnp.max(s, axis=1, keepdims=True))
        p = jnp.concatenate(
            [jnp.exp2(s[:, c * LANES:(c + 1) * LANES] - m_new).astype(BF16) for c in range(nck)], axis=1)
        acc_ref[...] = jnp.exp2(m_old - m_new) * acc_ref[...] + _dot(p, v_ref[0, pl.ds(start, tk), :])
        m_ref[...] = m_new
        return carry

    lax.fori_loop(0, n_tiles, attend, 0)

    outs = []
    for h in range(ATTN_HEADS):
        a = acc_ref[h * tq:(h + 1) * tq, :]
        outs.append(a[:, :HEAD_DIM] / a[:, HEAD_DIM:HEAD_DIM + 1])
    o_ref[0] = jnp.concatenate(outs, axis=1)


def _dsa_mixer(q, qi, wi, k, vext, ki, top_k, idx_scale, tq, tk):
    bsz, _, seq, _ = q.shape
    aw = ATTN_HEADS * HEAD_DIM
    nq = seq // tq
    blk = lambda b, i: (b, i, 0)
    heads = lambda b, i: (b, 0, i, 0)
    full = lambda b, i: (b, 0, 0)
    kern = functools.partial(_dsa_kernel, tq=tq, tk=tk, top_k=top_k, idx_scale=idx_scale,
                             pos_bits=max(1, (seq - 1).bit_length()))
    return pl.pallas_call(
        kern,
        grid=(bsz, nq),
        in_specs=[
            pl.BlockSpec((1, ATTN_HEADS, tq, HEAD_DIM), heads),
            pl.BlockSpec((1, IDX_HEADS, tq, IDX_DIM), heads),
            pl.BlockSpec((1, tq, wi.shape[-1]), blk),
            pl.BlockSpec((1, seq, k.shape[-1]), full),
            pl.BlockSpec((1, seq, vext.shape[-1]), full),
            pl.BlockSpec((1, seq, ki.shape[-1]), full),
        ],
        out_specs=pl.BlockSpec((1, tq, aw), blk),
        out_shape=jax.ShapeDtypeStruct((bsz, seq, aw), F32),
        scratch_shapes=[
            pltpu.VMEM((tq, seq), I32),
            pltpu.VMEM((IDX_HEADS * tq, LANES), F32),
            pltpu.VMEM((ATTN_HEADS * tq, LANES), F32),
            pltpu.VMEM((ATTN_HEADS * tq, LANES), F32),
        ],
        compiler_params=_cparams(("arbitrary", "arbitrary")),
        name="dsa_mixer",
    )(q, qi, wi, k, vext, ki)


def _gelu_tanh(x):
    return 0.5 * x * (1.0 + jnp.tanh(math.sqrt(2.0 / math.pi) * (x + 0.044715 * (x * x * x))))


def _out_kernel(x_ref, g_ref, mod_ref, ys_ref, ob_ref, wz_ref, wglu_ref, bglu_ref,
                wao_ref, wbo_ref, wout_ref, fg_ref, out_ref, *, sw, aw, final):
    x = x_ref[...]
    mod = mod_ref[0]
    h = _norm_mod(x, g_ref[...], mod).astype(BF16)
    z = _dot(h, wz_ref[...])
    d = x.shape[-1]
    za = z[:, :sw]
    zb = z[:, sw:sw + aw]
    ga = z[:, sw + aw:sw + aw + d]
    gb = z[:, sw + aw + d:]

    ya = _gelu_tanh(ys_ref[...])
    glu = _dot(ya.astype(BF16), wglu_ref[...]) + bglu_ref[...]
    ya = glu[:, :sw] * _sigmoid(glu[:, sw:])
    ya = _dot((ya * (za * _sigmoid(za))).astype(BF16), wao_ref[...])
    yb = _dot((ob_ref[...] * (zb * _sigmoid(zb))).astype(BF16), wbo_ref[...])
    merged = _sigmoid(ga) * ya + _sigmoid(gb) * yb
    y = x + mod[2:3, :] * _dot(merged.astype(BF16), wout_ref[...])
    if final:
        ms = jnp.mean(y * y, axis=-1, keepdims=True)
        y = y * lax.rsqrt(ms + RMS_EPS) * fg_ref[...]
    out_ref[...] = y


def _merge_out(x2, g, mod, ys, ob, wz, wglu, bglu, wao, wbo, wout, fg, rows_per_batch, tm, final):
    r, d = x2.shape
    sw = ys.shape[-1]
    aw = ob.shape[-1]
    tiles_per_batch = rows_per_batch // tm
    row = lambda i: (i, 0)
    const = lambda i: (0, 0)
    return pl.pallas_call(
        functools.partial(_out_kernel, sw=sw, aw=aw, final=final),
        grid=(r // tm,),
        in_specs=[
            pl.BlockSpec((tm, d), row),
            pl.BlockSpec((1, d), const),
            pl.BlockSpec((1, 3, d), lambda i: (i // tiles_per_batch, 0, 0)),
            pl.BlockSpec((tm, sw), row),
            pl.BlockSpec((tm, aw), row),
            pl.BlockSpec(wz.shape, const),
            pl.BlockSpec(wglu.shape, const),
            pl.BlockSpec((1, wglu.shape[-1]), const),
            pl.BlockSpec(wao.shape, const),
            pl.BlockSpec(wbo.shape, const),
            pl.BlockSpec(wout.shape, const),
            pl.BlockSpec((1, d), const),
        ],
        out_specs=pl.BlockSpec((tm, d), row),
        out_shape=jax.ShapeDtypeStruct((r, d), F32),
        compiler_params=_cparams(("arbitrary",)),
        name="merge_out",
    )(x2, g, mod, ys, ob, wz, wglu, bglu, wao, wbo, wout, fg)


def _split_points(d):
    sw = d // 2
    aw = ATTN_HEADS * HEAD_DIM
    iw = IDX_HEADS * IDX_DIM
    widths = (sw, sw, aw, HEAD_DIM, HEAD_DIM, aw, iw, IDX_DIM, IDX_HEADS, d, d)
    offs = [0]
    for w in widths:
        offs.append(offs[-1] + w)
    return widths, offs


def kernel(x, c, norm_g, w_mod, b_mod, w_in, a_re, a_im, b_re, b_im, c_re, c_im, d_skip, log_dt,
           w_glu, b_glu, w_a_o, w_b_o, w_out, final_g):
    bsz, seq, d = x.shape
    depth = w_in.shape[0]
    r = bsz * seq
    sw = d // 2
    aw = ATTN_HEADS * HEAD_DIM
    chunk = SSM_CHUNK
    top_k = min(TOPK_MAX, seq // 4)
    idx_scale = (IDX_HEADS * IDX_DIM) ** -0.5
    tm = min(512, seq)
    tq = min(128, seq)
    tk = min(512, seq)
    _, offs = _split_points(d)
    (o_xa, o_za, o_q, o_k, o_v, o_zb, o_qi, o_ki, o_wi, o_ga, o_gb, o_end) = offs

    mod_all = _modulation(c, w_mod, b_mod).reshape(depth, bsz, 3, d)
    x2 = x.reshape(r, d)
    for l in range(depth):
        w = w_in[l]
        small = jnp.concatenate([w[:, o_k:o_zb], w[:, o_ki:o_ga]], axis=1)
        pad = (-small.shape[1]) % LANES
        w_cat = jnp.concatenate([w[:, o_xa:o_za], w[:, o_q:o_k], w[:, o_qi:o_ki],
                                 jnp.pad(small, ((0, 0), (0, pad)))], axis=1).astype(BF16)
        widths = (sw, aw, IDX_HEADS * IDX_DIM, small.shape[1] + pad)
        g = norm_g[l].reshape(1, d)
        mod = mod_all[l]
        xa, q, qi, sm = _input_proj(x2, g, mod, w_cat, widths, seq, tm)

        winp, coutp, kl, lam = _s5_weights(a_re[l], a_im[l], b_re[l], b_im[l], c_re[l], c_im[l],
                                           log_dt[l], chunk)
        win, mt, wout, lam2 = _s5_layout_weights(winp, coutp, kl, lam)
        dsk = jnp.tile(d_skip[l].reshape(sw // LANES, 1, LANES), (1, 1, chunk))
        ys = _s5_mixer(xa, win, mt, wout, lam2, dsk, seq, chunk, min(SSM_ROWS, seq))

        kk = sm[:, 0:HEAD_DIM].astype(BF16).reshape(bsz, seq, HEAD_DIM)
        vv = sm[:, HEAD_DIM:2 * HEAD_DIM]
        vext = jnp.concatenate([vv, jnp.ones((r, 1), F32), jnp.zeros((r, LANES - HEAD_DIM - 1), F32)],
                               axis=1).astype(BF16).reshape(bsz, seq, LANES)
        ki = sm[:, 2 * HEAD_DIM:2 * HEAD_DIM + IDX_DIM].astype(BF16).reshape(bsz, seq, IDX_DIM)
        wi = sm[:, 2 * HEAD_DIM + IDX_DIM:2 * HEAD_DIM + IDX_DIM + IDX_HEADS].reshape(bsz, seq, IDX_HEADS)
        ob = _dsa_mixer(q, qi, wi, kk, vext, ki, top_k, idx_scale, tq, tk).reshape(r, aw)

        wz = jnp.concatenate([w[:, o_za:o_q], w[:, o_zb:o_qi], w[:, o_ga:o_end]], axis=1).astype(BF16)
        x2 = _merge_out(x2, g, mod, ys, ob, wz, w_glu[l].astype(BF16), b_glu[l].reshape(1, -1),
                        w_a_o[l].astype(BF16), w_b_o[l].astype(BF16), w_out[l].astype(BF16),
                        final_g.reshape(1, d), seq, tm, final=(l == depth - 1))
    return x2.reshape(bsz, seq, d)
```

```python
import functools
import math

import jax
import jax.numpy as jnp
from jax import lax
from jax.experimental import pallas as pl
from jax.experimental.pallas import tpu as pltpu

F32 = jnp.float32
BF16 = jnp.bfloat16
I32 = jnp.int32

RMS_EPS = 1e-6
SSM_GROUP = 16
SSM_STATE = 64
SSM_CHUNK = 8
SSM_ROWS = 2048
ATTN_HEADS = 8
HEAD_DIM = 64
IDX_HEADS = 8
IDX_DIM = 64
TOPK_MAX = 256

V7X_VMEM_LIMIT = 56 * 1024 * 1024
LANES = 128

NEG_BIG = -1e30
LOG2E = math.log2(math.e)
BISECT_STEPS_PER_TEST = 3
BISECT_MAX_STEPS = 400
NEG_INF = float("-inf")
F32_BIG = 3.0e38


def _cparams(sem):
    return pltpu.CompilerParams(dimension_semantics=sem, vmem_limit_bytes=V7X_VMEM_LIMIT)


def _sigmoid(x):
    return 1.0 / (1.0 + jnp.exp(-x))


def _dot(a, b, **kw):
    return jnp.dot(a, b, preferred_element_type=F32, **kw)


def _dot_nt(a, b):
    return lax.dot_general(a, b, (((1,), (1,)), ((), ())), preferred_element_type=F32)


def _mod_kernel(c_ref, w_ref, b_ref, o_ref):
    c = c_ref[...]
    sc = c * _sigmoid(c)
    o_ref[0] = _dot(sc, w_ref[0], precision=lax.Precision.HIGHEST) + b_ref[0]


def _modulation(c, w_mod, b_mod):
    depth, d, d3 = w_mod.shape
    bsz = c.shape[0]
    nj = d3 // d
    return pl.pallas_call(
        _mod_kernel,
        grid=(depth, nj),
        in_specs=[
            pl.BlockSpec((bsz, d), lambda l, j: (0, 0)),
            pl.BlockSpec((1, d, d), lambda l, j: (l, 0, j)),
            pl.BlockSpec((1, 1, d), lambda l, j: (l, 0, j)),
        ],
        out_specs=pl.BlockSpec((1, bsz, d), lambda l, j: (l, 0, j)),
        out_shape=jax.ShapeDtypeStruct((depth, bsz, d3), F32),
        compiler_params=_cparams(("arbitrary", "arbitrary")),
        name="adaln_mod",
    )(c, w_mod, b_mod.reshape(depth, 1, d3))


def _norm_mod(x, g, mod):
    ms = jnp.mean(x * x, axis=-1, keepdims=True)
    y = x * lax.rsqrt(ms + RMS_EPS) * g
    return y * (1.0 + mod[1:2, :]) + mod[0:1, :]


def _proj_kernel(x_ref, g_ref, mod_ref, w_ref, xa_ref, q_ref, qi_ref, sm_ref, *, widths):
    h = _norm_mod(x_ref[...], g_ref[...], mod_ref[0]).astype(BF16)
    p = _dot(h, w_ref[...])
    w_xa, w_q, w_qi, _ = widths
    o = 0
    xa_ref[...] = p[:, o:o + w_xa]
    o += w_xa
    for h in range(ATTN_HEADS):
        lo = o + h * HEAD_DIM
        q_ref[0, h] = (p[:, lo:lo + HEAD_DIM] * (HEAD_DIM ** -0.5 * LOG2E)).astype(BF16)
    o += w_q
    for h in range(IDX_HEADS):
        lo = o + h * IDX_DIM
        qi_ref[0, h] = p[:, lo:lo + IDX_DIM].astype(BF16)
    o += w_qi
    sm_ref[...] = p[:, o:]


def _input_proj(x2, g, mod, w_cat, widths, rows_per_batch, tm):
    r, d = x2.shape
    w_xa, w_q, w_qi, w_sm = widths
    tiles_per_batch = rows_per_batch // tm
    bsz = r // rows_per_batch
    row = lambda i: (i, 0)
    heads = lambda i: (i // tiles_per_batch, 0, i % tiles_per_batch, 0)
    return pl.pallas_call(
        functools.partial(_proj_kernel, widths=widths),
        grid=(r // tm,),
        in_specs=[
            pl.BlockSpec((tm, d), row),
            pl.BlockSpec((1, d), lambda i: (0, 0)),
            pl.BlockSpec((1, 3, d), lambda i: (i // tiles_per_batch, 0, 0)),
            pl.BlockSpec(w_cat.shape, lambda i: (0, 0)),
        ],
        out_specs=[
            pl.BlockSpec((tm, w_xa), row),
            pl.BlockSpec((1, ATTN_HEADS, tm, HEAD_DIM), heads),
            pl.BlockSpec((1, IDX_HEADS, tm, IDX_DIM), heads),
            pl.BlockSpec((tm, w_sm), row),
        ],
        out_shape=[
            jax.ShapeDtypeStruct((r, w_xa), F32),
            jax.ShapeDtypeStruct((bsz, ATTN_HEADS, rows_per_batch, HEAD_DIM), BF16),
            jax.ShapeDtypeStruct((bsz, IDX_HEADS, rows_per_batch, IDX_DIM), BF16),
            jax.ShapeDtypeStruct((r, w_sm), F32),
        ],
        compiler_params=_cparams(("arbitrary",)),
        name="input_proj",
    )(x2, g, mod, w_cat)


def _s5w_kernel(arc_ref, aic_ref, arr_ref, air_ref, ldt_ref, br_ref, bi_ref, cr_ref, ci_ref,
                winp_ref, coutp_ref, kl_ref, lam_ref, *, chunk):
    dt = jnp.exp(ldt_ref[0])

    def discretise(ar, ai):
        mag = jnp.exp(ar * dt)
        return mag * jnp.cos(ai * dt), mag * jnp.sin(ai * dt)

    ar, ai = arc_ref[0], aic_ref[0]
    lr, li = discretise(ar, ai)
    den = ar * ar + ai * ai
    nr = lr - 1.0
    zr = (nr * ar + li * ai) / den
    zi = (li * ar - nr * ai) / den
    br, bi = br_ref[0], bi_ref[0]
    bbr = zr * br - zi * bi
    bbi = zr * bi + zi * br

    rr, ri = discretise(arr_ref[0], air_ref[0])
    cr, ci = cr_ref[0], ci_ref[0]

    pr, pi = jnp.ones_like(lr), jnp.zeros_like(li)
    qr, qi = jnp.ones_like(rr), jnp.zeros_like(ri)
    hi = lax.Precision.HIGHEST
    for k in range(chunk):
        winp_ref[0, k, 0] = pr * bbr - pi * bbi
        winp_ref[0, k, 1] = pr * bbi + pi * bbr
        ckr = cr * qr - ci * qi
        cki = cr * qi + ci * qr
        kl_ref[0, k] = _dot(ckr, bbr, precision=hi) - _dot(cki, bbi, precision=hi)
        pr, pi = pr * lr - pi * li, pr * li + pi * lr
        qr, qi = qr * rr - qi * ri, qr * ri + qi * rr
        coutp_ref[0, k, 0] = cr * qr - ci * qi
        coutp_ref[0, k, 1] = -(cr * qi + ci * qr)
    lam_ref[0, 0] = qr
    lam_ref[0, 1] = qi


def _s5_weights(a_re, a_im, b_re, b_im, c_re, c_im, log_dt, chunk):
    g, n = a_re.shape
    p = b_re.shape[-1]
    g3 = lambda i: (i, 0, 0)
    col = pl.BlockSpec((1, n, 1), g3)
    rowb = pl.BlockSpec((1, 1, n), g3)
    return pl.pallas_call(
        functools.partial(_s5w_kernel, chunk=chunk),
        grid=(g,),
        in_specs=[col, col, rowb, rowb, pl.BlockSpec((1, 1, 1), g3),
                  pl.BlockSpec((1, n, p), g3), pl.BlockSpec((1, n, p), g3),
                  pl.BlockSpec((1, p, n), g3), pl.BlockSpec((1, p, n), g3)],
        out_specs=[
            pl.BlockSpec((1, chunk, 2, n, p), lambda i: (i, 0, 0, 0, 0)),
            pl.BlockSpec((1, chunk, 2, p, n), lambda i: (i, 0, 0, 0, 0)),
            pl.BlockSpec((1, chunk, p, p), lambda i: (i, 0, 0, 0)),
            pl.BlockSpec((1, 2, 1, n), lambda i: (i, 0, 0, 0)),
        ],
        out_shape=[
            jax.ShapeDtypeStruct((g, chunk, 2, n, p), F32),
            jax.ShapeDtypeStruct((g, chunk, 2, p, n), F32),
            jax.ShapeDtypeStruct((g, chunk, p, p), F32),
            jax.ShapeDtypeStruct((g, 2, 1, n), F32),
        ],
        compiler_params=_cparams(("arbitrary",)),
        name="s5_discretise",
    )(a_re.reshape(g, n, 1), a_im.reshape(g, n, 1), a_re.reshape(g, 1, n), a_im.reshape(g, 1, n),
      log_dt.reshape(g, 1, 1), b_re, b_im, c_re, c_im)


def _s5_layout_weights(winp, coutp, kl, lam):
    g, t, _, n, p = winp.shape
    gl = LANES // p
    nl = g // gl
    eye = jnp.eye(gl, dtype=F32)
    a = jnp.flip(winp, axis=1).reshape(nl, gl, t, 2, n, p)
    win = jnp.einsum('lgsrnq,gh->lsgqrhn', a, eye).reshape(nl, t * LANES, 2 * gl * n)
    s_idx = jnp.arange(t)[:, None]
    t_idx = jnp.arange(t)[None, :]
    d = t_idx - s_idx
    kk = kl[:, jnp.clip(d, 0, t - 1)]
    kk = jnp.where((d >= 0)[None, :, :, None, None], kk, 0.0).reshape(nl, gl, t, t, p, p)
    mt = jnp.einsum('lgstpq,gh->lsgqthp', kk, eye).reshape(nl, t * LANES, t * LANES)
    c = coutp.reshape(nl, gl, t, 2, p, n)
    wout = jnp.einsum('lgtrpn,gh->lrhntgp', c, eye).reshape(nl, 2, gl * n, t * LANES)
    lam2 = lam.reshape(nl, gl, 2, n).transpose(0, 2, 1, 3).reshape(nl, 2, 1, gl * n)
    return win.astype(BF16), mt.astype(BF16), wout.astype(BF16), lam2


def _s5_kernel(x_ref, win_ref, mt_ref, wout_ref, lam_ref, dsk_ref, y_ref,
               xr_ref, xi_ref, er_ref, ei_ref, sr_ref, si_ref, *, chunk, nchunks):
    ns = sr_ref.shape[-1]

    @pl.when(pl.program_id(2) == 0)
    def _():
        sr_ref[...] = jnp.zeros(sr_ref.shape, F32)
        si_ref[...] = jnp.zeros(si_ref.shape, F32)

    u = jnp.concatenate([x_ref[pl.ds(s, nchunks, stride=chunk), :] for s in range(chunk)], axis=1)
    ub = u.astype(BF16)
    x = _dot(ub, win_ref[0])
    xr_ref[...] = x[:, :ns]
    xi_ref[...] = x[:, ns:]
    ar = lam_ref[0, 0]
    ai = lam_ref[0, 1]

    def step(c, carry):
        er, ei = carry
        er_ref[pl.ds(c, 1), :] = er
        ei_ref[pl.ds(c, 1), :] = ei
        return (er * ar - ei * ai + xr_ref[pl.ds(c, 1), :],
                ei * ar + er * ai + xi_ref[pl.ds(c, 1), :])

    er, ei = lax.fori_loop(0, nchunks, step, (sr_ref[...], si_ref[...]), unroll=8)
    sr_ref[...] = er
    si_ref[...] = ei

    y = (_dot(ub, mt_ref[0]) + _dot(er_ref[...].astype(BF16), wout_ref[0, 0])
         + _dot(ei_ref[...].astype(BF16), wout_ref[0, 1]) + u * dsk_ref[0])
    for s in range(chunk):
        y_ref[pl.ds(s, nchunks, stride=chunk), :] = y[:, s * LANES:(s + 1) * LANES]


def _s5_mixer(xa, win, mt, wout, lam2, dsk, rows_per_batch, chunk, rows):
    r, sw = xa.shape
    nl = sw // LANES
    bsz = r // rows_per_batch
    nb = rows_per_batch // rows
    nchunks = rows // chunk
    ns = lam2.shape[-1]
    tile = lambda l, b, j: (l, 0, 0)
    tile4 = lambda l, b, j: (l, 0, 0, 0)
    data = lambda l, b, j: (b * nb + j, l)
    return pl.pallas_call(
        functools.partial(_s5_kernel, chunk=chunk, nchunks=nchunks),
        grid=(nl, bsz, nb),
        in_specs=[
            pl.BlockSpec((rows, LANES), data),
            pl.BlockSpec((1,) + win.shape[1:], tile),
            pl.BlockSpec((1,) + mt.shape[1:], tile),
            pl.BlockSpec((1,) + wout.shape[1:], tile4),
            pl.BlockSpec((1,) + lam2.shape[1:], tile4),
            pl.BlockSpec((1,) + dsk.shape[1:], tile),
        ],
        out_specs=pl.BlockSpec((rows, LANES), data),
        out_shape=jax.ShapeDtypeStruct((r, sw), F32),
        scratch_shapes=[pltpu.VMEM((nchunks, ns), F32) for _ in range(4)]
        + [pltpu.VMEM((1, ns), F32) for _ in range(2)],
        compiler_params=_cparams(("arbitrary", "arbitrary", "arbitrary")),
        name="s5_mixer",
    )(xa, win, mt, wout, lam2, dsk)


def _row_any(flag):
    return jnp.max(flag.astype(I32))


def _dsa_kernel(q_ref, qi_ref, wi_ref, k_ref, v_ref, ki_ref, o_ref,
                key_ref, m_ref, acc_ref, *, tq, tk, top_k, idx_scale, pos_bits):
    i = pl.program_id(1)
    n_tiles = (i * tq + tq + tk - 1) // tk
    half = tk // 2
    q_pos = i * tq + lax.broadcasted_iota(I32, (half, tq), 1)
    row_h = lax.broadcasted_iota(I32, (half, tq), 0)
    row = lax.broadcasted_iota(I32, (tk, tq), 0)
    nck = tk // LANES

    wi = wi_ref[0] * idx_scale

    def score_half(start, gmax):
        y = _dot_nt(ki_ref[0, pl.ds(start, half), :], qi_ref[0].reshape(IDX_HEADS * tq, IDX_DIM))
        sc = jnp.maximum(y[:, 0:tq], 0.0) * wi[0:1, :]
        for h in range(1, IDX_HEADS):
            sc = sc + jnp.maximum(y[:, h * tq:(h + 1) * tq], 0.0) * wi[h:h + 1, :]
        key = jnp.where(start + row_h <= q_pos, sc, NEG_INF)
        key_ref[pl.ds(start, half), :] = key
        return jnp.maximum(gmax, key)

    def score_tile(t, gmax):
        start = pl.multiple_of(t * tk, tk)
        return score_half(start + half, score_half(start, gmax))

    gmax = lax.fori_loop(0, n_tiles, score_tile, jnp.full((half, tq), NEG_INF, F32))

    lo0 = jnp.maximum(jnp.min(gmax, axis=0, keepdims=True), -F32_BIG)
    top = jnp.max(gmax, axis=0, keepdims=True)
    hi0 = top + (jnp.abs(top) * 2.0 ** -20 + 1e-30)

    def fold(hits):
        return jnp.sum(hits.reshape(tk // 8, 8, tq), axis=0)

    def count_tiles(pred):
        def body(t, cnt):
            start = pl.multiple_of(t * tk, tk)
            return cnt + fold(pred(key_ref[pl.ds(start, tk), :], start).astype(I32))
        cnt = lax.fori_loop(0, n_tiles, body, jnp.zeros((8, tq), I32))
        return jnp.sum(cnt, axis=0, keepdims=True)

    def count_ge(cand):
        return count_tiles(lambda kt, start: kt >= cand)

    def bisect_cond(st):
        return st[0] > 0

    def bisect_body(st):
        _, it, lo, hi, thr, done = st
        for _ in range(BISECT_STEPS_PER_TEST):
            mid = 0.5 * lo + 0.5 * hi
            inside = jnp.logical_and(mid > lo, mid < hi)
            cnt = count_ge(mid)
            hit = jnp.logical_and(jnp.logical_and(cnt == top_k, done == 0), inside)
            lo = jnp.where(jnp.logical_and(cnt >= top_k, inside), mid, lo)
            hi = jnp.where(jnp.logical_and(cnt < top_k, inside), mid, hi)
            thr = jnp.where(hit, mid, thr)
            done = jnp.where(hit, 1, done)
        mid = 0.5 * lo + 0.5 * hi
        active = jnp.logical_and(done == 0, jnp.logical_and(mid > lo, mid < hi))
        go = jnp.where(it < BISECT_MAX_STEPS, (jnp.max(active.astype(F32)) > 0.0).astype(I32), 0)
        return go, it + BISECT_STEPS_PER_TEST, lo, hi, thr, done

    zcol = jnp.zeros((1, tq), I32)
    _, _, tau, _, thr, done = lax.while_loop(
        bisect_cond, bisect_body,
        (jnp.int32(1), jnp.int32(0), lo0, hi0, jnp.zeros((1, tq), F32), zcol))

    open_rows = done == 0
    thr = jnp.where(done == 1, thr, tau)

    @pl.when(_row_any(open_rows) > 0)
    def _():
        need = top_k - count_tiles(lambda kt, start: kt > tau)
        tied = jnp.logical_and(open_rows, count_tiles(lambda kt, start: kt == tau) > need)

        def bisect_pos(p, last):
            cand = last + jnp.left_shift(jnp.int32(1), pos_bits - 1 - p)
            below = count_tiles(lambda kt, start: jnp.logical_and(kt == tau, start + row < cand))
            return jnp.where(below < need, cand, last)

        last = lax.fori_loop(0, pos_bits, bisect_pos, zcol)

        def demote(t, carry):
            start = pl.multiple_of(t * tk, tk)
            kt = key_ref[pl.ds(start, tk), :]
            drop = jnp.logical_and(jnp.logical_and(kt == tau, start + row > last), tied)
            key_ref[pl.ds(start, tk), :] = jnp.where(drop, NEG_INF, kt)
            return carry

        lax.fori_loop(0, n_tiles, demote, 0)

    m_ref[...] = jnp.full(m_ref.shape, NEG_BIG, F32)
    acc_ref[...] = jnp.zeros(acc_ref.shape, F32)

    def attend(t, carry):
        start = pl.multiple_of(t * tk, tk)
        kt = key_ref[pl.ds(start, tk), :]
        keep = jnp.logical_and(kt >= thr, kt > NEG_INF)
        bias = jnp.where(keep, 0.0, NEG_BIG).T
        s = _dot_nt(q_ref[0].reshape(ATTN_HEADS * tq, HEAD_DIM), k_ref[0, pl.ds(start, tk), :])
        s = (s.reshape(ATTN_HEADS, tq, tk) + bias[None]).reshape(ATTN_HEADS * tq, tk)
        m_old = m_ref[...]
        m_new = jnp.maximum(m_old, jnp.max(s, axis=1, keepdims=True))
        p = jnp.concatenate(
            [jnp.exp2(s[:, c * LANES:(c + 1) * LANES] - m_new).astype(BF16) for c in range(nck)], axis=1)
        acc_ref[...] = jnp.exp2(m_old - m_new) * acc_ref[...] + _dot(p, v_ref[0, pl.ds(start, tk), :])
        m_ref[...] = m_new
        return carry

    lax.fori_loop(0, n_tiles, attend, 0)

    outs = []
    for h in range(ATTN_HEADS):
        a = acc_ref[h * tq:(h + 1) * tq, :]
        outs.append(a[:, :HEAD_DIM] / a[:, HEAD_DIM:HEAD_DIM + 1])
    o_ref[0] = jnp.concatenate(outs, axis=1)


def _dsa_mixer(q, qi, wi, k, vext, ki, top_k, idx_scale, tq, tk):
    bsz, _, seq, _ = q.shape
    aw = ATTN_HEADS * HEAD_DIM
    nq = seq // tq
    blk = lambda b, i: (b, i, 0)
    heads = lambda b, i: (b, 0, i, 0)
    full = lambda b, i: (b, 0, 0)
    kern = functools.partial(_dsa_kernel, tq=tq, tk=tk, top_k=top_k, idx_scale=idx_scale,
                             pos_bits=max(1, (seq - 1).bit_length()))
    return pl.pallas_call(
        kern,
        grid=(bsz, nq),
        in_specs=[
            pl.BlockSpec((1, ATTN_HEADS, tq, HEAD_DIM), heads),
            pl.BlockSpec((1, IDX_HEADS, tq, IDX_DIM), heads),
            pl.BlockSpec((1, IDX_HEADS, tq), lambda b, i: (b, 0, i)),
            pl.BlockSpec((1, seq, k.shape[-1]), full),
            pl.BlockSpec((1, seq, vext.shape[-1]), full),
            pl.BlockSpec((1, seq, ki.shape[-1]), full),
        ],
        out_specs=pl.BlockSpec((1, tq, aw), blk),
        out_shape=jax.ShapeDtypeStruct((bsz, seq, aw), F32),
        scratch_shapes=[
            pltpu.VMEM((seq, tq), F32),
            pltpu.VMEM((ATTN_HEADS * tq, LANES), F32),
            pltpu.VMEM((ATTN_HEADS * tq, LANES), F32),
        ],
        compiler_params=_cparams(("arbitrary", "arbitrary")),
        name="dsa_mixer",
    )(q, qi, wi, k, vext, ki)


def _gelu_tanh(x):
    return 0.5 * x * (1.0 + jnp.tanh(math.sqrt(2.0 / math.pi) * (x + 0.044715 * (x * x * x))))


def _out_kernel(x_ref, g_ref, mod_ref, ys_ref, ob_ref, wz_ref, wglu_ref, bglu_ref,
                wao_ref, wbo_ref, wout_ref, fg_ref, out_ref, *, sw, aw, final):
    x = x_ref[...]
    mod = mod_ref[0]
    h = _norm_mod(x, g_ref[...], mod).astype(BF16)
    z = _dot(h, wz_ref[...])
    d = x.shape[-1]
    za = z[:, :sw]
    zb = z[:, sw:sw + aw]
    ga = z[:, sw + aw:sw + aw + d]
    gb = z[:, sw + aw + d:]

    ya = _gelu_tanh(ys_ref[...])
    glu = _dot(ya.astype(BF16), wglu_ref[...]) + bglu_ref[...]
    ya = glu[:, :sw] * _sigmoid(glu[:, sw:])
    ya = _dot((ya * (za * _sigmoid(za))).astype(BF16), wao_ref[...])
    yb = _dot((ob_ref[...] * (zb * _sigmoid(zb))).astype(BF16), wbo_ref[...])
    merged = _sigmoid(ga) * ya + _sigmoid(gb) * yb
    y = x + mod[2:3, :] * _dot(merged.astype(BF16), wout_ref[...])
    if final:
        ms = jnp.mean(y * y, axis=-1, keepdims=True)
        y = y * lax.rsqrt(ms + RMS_EPS) * fg_ref[...]
    out_ref[...] = y


def _merge_out(x2, g, mod, ys, ob, wz, wglu, bglu, wao, wbo, wout, fg, rows_per_batch, tm, final):
    r, d = x2.shape
    sw = ys.shape[-1]
    aw = ob.shape[-1]
    tiles_per_batch = rows_per_batch // tm
    row = lambda i: (i, 0)
    const = lambda i: (0, 0)
    return pl.pallas_call(
        functools.partial(_out_kernel, sw=sw, aw=aw, final=final),
        grid=(r // tm,),
        in_specs=[
            pl.BlockSpec((tm, d), row),
            pl.BlockSpec((1, d), const),
            pl.BlockSpec((1, 3, d), lambda i: (i // tiles_per_batch, 0, 0)),
            pl.BlockSpec((tm, sw), row),
            pl.BlockSpec((tm, aw), row),
            pl.BlockSpec(wz.shape, const),
            pl.BlockSpec(wglu.shape, const),
            pl.BlockSpec((1, wglu.shape[-1]), const),
            pl.BlockSpec(wao.shape, const),
            pl.BlockSpec(wbo.shape, const),
            pl.BlockSpec(wout.shape, const),
            pl.BlockSpec((1, d), const),
        ],
        out_specs=pl.BlockSpec((tm, d), row),
        out_shape=jax.ShapeDtypeStruct((r, d), F32),
        compiler_params=_cparams(("arbitrary",)),
        name="merge_out",
    )(x2, g, mod, ys, ob, wz, wglu, bglu, wao, wbo, wout, fg)


def _split_points(d):
    sw = d // 2
    aw = ATTN_HEADS * HEAD_DIM
    iw = IDX_HEADS * IDX_DIM
    widths = (sw, sw, aw, HEAD_DIM, HEAD_DIM, aw, iw, IDX_DIM, IDX_HEADS, d, d)
    offs = [0]
    for w in widths:
        offs.append(offs[-1] + w)
    return widths, offs


def kernel(x, c, norm_g, w_mod, b_mod, w_in, a_re, a_im, b_re, b_im, c_re, c_im, d_skip, log_dt,
           w_glu, b_glu, w_a_o, w_b_o, w_out, final_g):
    bsz, seq, d = x.shape
    depth = w_in.shape[0]
    r = bsz * seq
    sw = d // 2
    aw = ATTN_HEADS * HEAD_DIM
    chunk = SSM_CHUNK
    top_k = min(TOPK_MAX, seq // 4)
    idx_scale = (IDX_HEADS * IDX_DIM) ** -0.5
    tm = min(512, seq)
    tq = min(128, seq)
    tk = min(512, seq)
    _, offs = _split_points(d)
    (o_xa, o_za, o_q, o_k, o_v, o_zb, o_qi, o_ki, o_wi, o_ga, o_gb, o_end) = offs

    mod_all = _modulation(c, w_mod, b_mod).reshape(depth, bsz, 3, d)
    x2 = x.reshape(r, d)
    for l in range(depth):
        w = w_in[l]
        small = jnp.concatenate([w[:, o_k:o_zb], w[:, o_ki:o_ga]], axis=1)
        pad = (-small.shape[1]) % LANES
        w_cat = jnp.concatenate([w[:, o_xa:o_za], w[:, o_q:o_k], w[:, o_qi:o_ki],
                                 jnp.pad(small, ((0, 0), (0, pad)))], axis=1).astype(BF16)
        widths = (sw, aw, IDX_HEADS * IDX_DIM, small.shape[1] + pad)
        g = norm_g[l].reshape(1, d)
        mod = mod_all[l]
        xa, q, qi, sm = _input_proj(x2, g, mod, w_cat, widths, seq, tm)

        winp, coutp, kl, lam = _s5_weights(a_re[l], a_im[l], b_re[l], b_im[l], c_re[l], c_im[l],
                                           log_dt[l], chunk)
        win, mt, wout, lam2 = _s5_layout_weights(winp, coutp, kl, lam)
        dsk = jnp.tile(d_skip[l].reshape(sw // LANES, 1, LANES), (1, 1, chunk))
        ys = _s5_mixer(xa, win, mt, wout, lam2, dsk, seq, chunk, min(SSM_ROWS, seq))

        kk = sm[:, 0:HEAD_DIM].astype(BF16).reshape(bsz, seq, HEAD_DIM)
        vv = sm[:, HEAD_DIM:2 * HEAD_DIM]
        vext = jnp.concatenate([vv, jnp.ones((r, 1), F32), jnp.zeros((r, LANES - HEAD_DIM - 1), F32)],
                               axis=1).astype(BF16).reshape(bsz, seq, LANES)
        ki = sm[:, 2 * HEAD_DIM:2 * HEAD_DIM + IDX_DIM].astype(BF16).reshape(bsz, seq, IDX_DIM)
        wi = sm[:, 2 * HEAD_DIM + IDX_DIM:2 * HEAD_DIM + IDX_DIM + IDX_HEADS].reshape(bsz, seq, IDX_HEADS)
        wi = wi.transpose(0, 2, 1)
        ob = _dsa_mixer(q, qi, wi, kk, vext, ki, top_k, idx_scale, tq, tk).reshape(r, aw)

        wz = jnp.concatenate([w[:, o_za:o_q], w[:, o_zb:o_qi], w[:, o_ga:o_end]], axis=1).astype(BF16)
        x2 = _merge_out(x2, g, mod, ys, ob, wz, w_glu[l].astype(BF16), b_glu[l].reshape(1, -1),
                        w_a_o[l].astype(BF16), w_b_o[l].astype(BF16), w_out[l].astype(BF16),
                        final_g.reshape(1, d), seq, tm, final=(l == depth - 1))
    return x2.reshape(bsz, seq, d)
```
